```python
import math
import jax, jax.numpy as jnp
from jax import lax
import numpy as np

D_MODEL = 2048
BATCH = 8
SEQ = 2048
DEPTH = 1
DEC_BATCH = 32
DEC_SEQ = 8
PAST_LEN = 8192
PAGE_SIZE = 128

ATT_HEADS = 16
ATT_HEAD_DIM = 64
D_ATT = ATT_HEADS * ATT_HEAD_DIM
Q_BLOCK = 128
D_SSM = D_MODEL
SSM_HEAD_DIM = 64
SSM_HEADS = D_SSM // SSM_HEAD_DIM
SSM_GROUPS = 8
SSM_STATE = 128
CONV_WIDTH = 4
CONV_DIM = D_SSM + 2 * SSM_GROUPS * SSM_STATE
SSD_CHUNK = 128
PEER_HEADS = 8
N_KEYS = 128
N_EXPERTS = N_KEYS * N_KEYS
PK_DIM = 256
PK_HALF = PK_DIM // 2
PK_TOPK = 16
PEER_BLOCK = 128
IN_SIZES = (D_ATT, D_ATT, D_ATT, ATT_HEADS, D_SSM, CONV_DIM, SSM_HEADS, D_MODEL, D_MODEL)
N_IN = 3 * D_ATT + ATT_HEADS + D_SSM + CONV_DIM + SSM_HEADS + 2 * D_MODEL
DEEPNORM_ALPHA = (2 * DEPTH) ** 0.25
DEEPNORM_BETA = (8 * DEPTH) ** -0.25
NORM_EPS = 1e-5
CACHE_FORGET_BIAS = 7.0

kernel_name = 'fox_mamba2_peer_hybrid_step'


def layer_norm(x, g, b):
    xf = x.astype(jnp.float32)
    mu = jnp.mean(xf, -1, keepdims=True)
    var = jnp.mean(jnp.square(xf - mu), -1, keepdims=True)
    return ((xf - mu) * lax.rsqrt(var + NORM_EPS)).astype(x.dtype) * g + b


def split_in(u):
    offs, acc = [], 0
    for s in IN_SIZES[:-1]:
        acc += s
        offs.append(acc)
    return jnp.split(u, offs, axis=-1)


def mixer_inputs(x, w_in, b_fgate):
    bsz, l = x.shape[:2]
    q, k, v, fg, z, xbc, dt_raw, ga, gs = split_in(x @ w_in)
    shp = (bsz, l, ATT_HEADS, ATT_HEAD_DIM)
    logf = jax.nn.log_sigmoid(fg.astype(jnp.float32) + b_fgate)
    return q.reshape(shp), k.reshape(shp), v.reshape(shp), logf, z, xbc, dt_raw, ga, gs


def fox_prompt(q, k, v, logf):
    b, s, h, dh = q.shape
    scale = dh ** -0.5
    cum_t = jnp.transpose(lax.cumsum(logf, axis=1), (0, 2, 1))
    kpos = jnp.arange(s)

    def block(i):
        start = i * Q_BLOCK
        qb = lax.dynamic_slice_in_dim(q, start, Q_BLOCK, axis=1)
        cq = lax.dynamic_slice_in_dim(cum_t, start, Q_BLOCK, axis=2)
        logits = jnp.einsum('bqhd,bkhd->bhqk', qb, k).astype(jnp.float32) * scale
        logits = logits + cq[..., :, None] - cum_t[..., None, :]
        qpos = start + jnp.arange(Q_BLOCK)
        logits = jnp.where(kpos[None, :] <= qpos[:, None], logits, -jnp.inf)
        p = jax.nn.softmax(logits, axis=-1).astype(v.dtype)
        return jnp.einsum('bhqk,bkhd->bqhd', p, v)

    out = lax.map(block, jnp.arange(s // Q_BLOCK))
    return jnp.moveaxis(out, 0, 1).reshape(b, s, h, dh)


def fox_sample(q, k, v, logf, k_past, v_past, logf_past):
    l = q.shape[1]
    past = k_past.shape[1]
    scale = q.shape[-1] ** -0.5
    lp32 = logf_past.astype(jnp.float32)
    rev = lax.cumsum(lp32, axis=1, reverse=True) - lp32
    rev_t = jnp.transpose(rev, (0, 2, 1))
    cn_t = jnp.transpose(lax.cumsum(logf, axis=1), (0, 2, 1))
    lg_past = jnp.einsum('bqhd,bkhd->bhqk', q, k_past).astype(jnp.float32) * scale
    lg_past = lg_past + cn_t[..., :, None] + rev_t[..., None, :]
    lg_new = jnp.einsum('bqhd,bkhd->bhqk', q, k).astype(jnp.float32) * scale
    lg_new = lg_new + cn_t[..., :, None] - cn_t[..., None, :]
    causal = jnp.tril(jnp.ones((l, l), dtype=bool))
    lg_new = jnp.where(causal, lg_new, -jnp.inf)
    p = jax.nn.softmax(jnp.concatenate([lg_past, lg_new], axis=-1), axis=-1).astype(v.dtype)
    return (jnp.einsum('bhqk,bkhd->bqhd', p[..., :past], v_past)
            + jnp.einsum('bhqk,bkhd->bqhd', p[..., past:], v))


def causal_conv(xbc, buf, w, b):
    l = xbc.shape[1]
    xp = jnp.concatenate([buf.astype(xbc.dtype), xbc], axis=1)
    y = b
    for j in range(CONV_WIDTH):
        y = y + w[j] * xp[:, j:j + l]
    return jax.nn.silu(y), xp[:, l:]


def ssd(x, dt, a, bm, cm, h0):
    bsz, l = x.shape[:2]
    t = math.gcd(SSD_CHUNK, l)
    nc = l // t
    r = SSM_HEADS // SSM_GROUPS
    xc = (x * dt[..., None]).reshape(bsz, nc, t, SSM_GROUPS, r, SSM_HEAD_DIM)
    cs = lax.cumsum((dt * a).reshape(bsz, nc, t, SSM_GROUPS, r), axis=2)
    bc = bm.reshape(bsz, nc, t, SSM_GROUPS, SSM_STATE)
    cc = cm.reshape(bsz, nc, t, SSM_GROUPS, SSM_STATE)
    seg = cs[:, :, :, None] - cs[:, :, None, :]
    causal = jnp.tril(jnp.ones((t, t), dtype=bool))[:, :, None, None]
    decay = jnp.where(causal, jnp.exp(jnp.where(causal, seg, 0.0)), 0.0)
    cb = jnp.einsum('bctgn,bcsgn->bctsg', cc, bc)
    y_diag = jnp.einsum('bctsg,bctsgr,bcsgrp->bctgrp', cb, decay, xc)
    decay_end = jnp.exp(cs[:, :, -1:] - cs)
    states = jnp.einsum('bcsgn,bcsgr,bcsgrp->bcgrpn', bc, decay_end, xc)
    chunk_decay = jnp.exp(cs[:, :, -1])

    def step(h, inp):
        st, dec = inp
        return h * dec[..., None, None] + st, h

    h0g = h0.reshape(bsz, SSM_GROUPS, r, SSM_HEAD_DIM, SSM_STATE)
    h_fin, h_prev = lax.scan(step, h0g, (jnp.moveaxis(states, 1, 0), jnp.moveaxis(chunk_decay, 1, 0)))
    h_prev = jnp.moveaxis(h_prev, 0, 1)
    y_off = jnp.einsum('bctgn,bcgrpn,bctgr->bctgrp', cc, h_prev, jnp.exp(cs))
    y = (y_diag + y_off).reshape(bsz, l, SSM_HEADS, SSM_HEAD_DIM)
    return y, h_fin.reshape(bsz, SSM_HEADS, SSM_HEAD_DIM, SSM_STATE)


def mamba_branch(z, xbc, dt_raw, conv_buf, h0, conv_w, conv_b, dt_bias, a_log, d_skip, norm_w):
    bsz, l = z.shape[:2]
    xbc, new_buf = causal_conv(xbc, conv_buf, conv_w, conv_b)
    xs, bm, cm = jnp.split(xbc, [D_SSM, D_SSM + SSM_GROUPS * SSM_STATE], axis=-1)
    xs = xs.reshape(bsz, l, SSM_HEADS, SSM_HEAD_DIM)
    dt = jax.nn.softplus(dt_raw.astype(jnp.float32) + dt_bias)
    a = -jnp.exp(a_log.astype(jnp.float32))
    y, h_new = ssd(xs, dt, a, bm.reshape(bsz, l, SSM_GROUPS, SSM_STATE),
                   cm.reshape(bsz, l, SSM_GROUPS, SSM_STATE), h0.astype(jnp.float32))
    y = (y + d_skip[:, None] * xs).reshape(bsz, l, D_SSM) * jax.nn.silu(z)
    yg = y.reshape(bsz, l, SSM_GROUPS, D_SSM // SSM_GROUPS).astype(jnp.float32)
    yg = yg * lax.rsqrt(jnp.mean(jnp.square(yg), -1, keepdims=True) + NORM_EPS)
    return yg.reshape(bsz, l, D_SSM).astype(z.dtype) * norm_w, new_buf, h_new


def merge_branches(x, att, ssm, ga, gs, w_br_att, w_br_ssm, w_out, g, b):
    bsz, l = x.shape[:2]
    mixed = (jax.nn.sigmoid(ga) * (att.reshape(bsz, l, D_ATT) @ w_br_att)
             + jax.nn.sigmoid(gs) * (ssm @ w_br_ssm))
    return layer_norm(DEEPNORM_ALPHA * x + mixed @ w_out, g, b)


def peer(x, w_query, sub_keys, expert_u, expert_v):
    shp = x.shape
    xt = x.reshape(-1, D_MODEL)
    n = xt.shape[0]
    n_pad = -(-n // PEER_BLOCK) * PEER_BLOCK
    xt = jnp.pad(xt, ((0, n_pad - n), (0, 0)))

    def block(xb):
        tn = xb.shape[0]
        q = (xb @ w_query).reshape(tn, PEER_HEADS, 2, PK_HALF)
        s = jnp.einsum('thkd,hknd->thkn', q, sub_keys).astype(jnp.float32)
        top_s, top_i = lax.top_k(s, PK_TOPK)
        cand_s = (top_s[:, :, 0, :, None] + top_s[:, :, 1, None, :]).reshape(tn, PEER_HEADS, PK_TOPK * PK_TOPK)
        cand_i = (top_i[:, :, 0, :, None] * N_KEYS + top_i[:, :, 1, None, :]).reshape(tn, PEER_HEADS, PK_TOPK * PK_TOPK)
        best_s, best_j = lax.top_k(cand_s, PK_TOPK)
        experts = jnp.take_along_axis(cand_i, best_j, axis=-1)
        gates = jax.nn.softmax(best_s, axis=-1).astype(xb.dtype)
        act = jax.nn.gelu(jnp.einsum('thkd,td->thk', expert_u[experts], xb), approximate=False)
        return jnp.einsum('thk,thkd->td', gates * act, expert_v[experts])

    out = lax.map(block, xt.reshape(-1, PEER_BLOCK, D_MODEL))
    return out.reshape(n_pad, D_MODEL)[:n].reshape(shp)


def setup_inputs(seed: int = 0) -> dict:
    key = jax.random.key(seed)
    ks = jax.random.split(key, 32)
    f32 = jnp.float32
    n_pages = PAST_LEN // PAGE_SIZE
    n_pool = (5 * DEC_BATCH * n_pages + 3) // 4

    def nrm(k, shape, s=1.0):
        return s * jax.random.normal(k, shape, f32)

    x_prompt = nrm(ks[0], (BATCH, SEQ, D_MODEL))
    x_sample = nrm(ks[1], (DEC_BATCH, DEC_SEQ, D_MODEL))
    cache_k = nrm(ks[2], (DEPTH, n_pool, PAGE_SIZE, ATT_HEADS, ATT_HEAD_DIM))
    cache_v = nrm(ks[3], (DEPTH, n_pool, PAGE_SIZE, ATT_HEADS, ATT_HEAD_DIM))
    cache_logf = jax.nn.log_sigmoid(CACHE_FORGET_BIAS + nrm(ks[4], (DEPTH, n_pool, PAGE_SIZE, ATT_HEADS), 0.5))
    state_ssm = nrm(ks[5], (DEPTH, DEC_BATCH, SSM_HEADS, SSM_HEAD_DIM, SSM_STATE), 0.1)
    state_conv = nrm(ks[6], (DEPTH, DEC_BATCH, CONV_WIDTH - 1, CONV_DIM))
    page_table = jax.random.permutation(ks[7], n_pool)[: DEC_BATCH * n_pages].reshape(DEC_BATCH, n_pages).astype(jnp.int32)
    w_in = nrm(ks[8], (DEPTH, D_MODEL, N_IN), D_MODEL ** -0.5)
    b_fgate = jax.random.uniform(ks[9], (DEPTH, ATT_HEADS), f32, 1.0, 4.0)
    conv_w = nrm(ks[10], (DEPTH, CONV_WIDTH, CONV_DIM), CONV_WIDTH ** -0.5)
    conv_b = nrm(ks[11], (DEPTH, CONV_DIM), 0.01)
    dt0 = jnp.exp(jax.random.uniform(ks[12], (DEPTH, SSM_HEADS), f32, math.log(1e-3), math.log(1e-1)))
    dt_bias = dt0 + jnp.log(-jnp.expm1(-dt0))
    a_log = jnp.log(jax.random.uniform(ks[13], (DEPTH, SSM_HEADS), f32, 1.0, 16.0))
    d_skip = 1.0 + nrm(ks[14], (DEPTH, SSM_HEADS), 0.1)
    ssm_norm_w = 1.0 + nrm(ks[15], (DEPTH, D_SSM), 0.1)
    w_br_att = nrm(ks[16], (DEPTH, D_ATT, D_MODEL), D_ATT ** -0.5)
    w_br_ssm = nrm(ks[17], (DEPTH, D_SSM, D_MODEL), D_SSM ** -0.5)
    w_out = nrm(ks[18], (DEPTH, D_MODEL, D_MODEL), DEEPNORM_BETA * D_MODEL ** -0.5)
    ln1_g = 1.0 + nrm(ks[19], (DEPTH, D_MODEL), 0.1)
    ln1_b = nrm(ks[20], (DEPTH, D_MODEL), 0.01)
    w_query = nrm(ks[21], (DEPTH, D_MODEL, PEER_HEADS * PK_DIM), D_MODEL ** -0.5)
    sub_keys = nrm(ks[22], (DEPTH, PEER_HEADS, 2, N_KEYS, PK_HALF), PK_HALF ** -0.5)
    expert_u = nrm(ks[23], (DEPTH, N_EXPERTS, D_MODEL), D_MODEL ** -0.5)
    expert_v = nrm(ks[24], (DEPTH, N_EXPERTS, D_MODEL), DEEPNORM_BETA)
    ln2_g = 1.0 + nrm(ks[25], (DEPTH, D_MODEL), 0.1)
    ln2_b = nrm(ks[26], (DEPTH, D_MODEL), 0.01)
    return {'x_prompt': x_prompt, 'x_sample': x_sample, 'cache_k': cache_k, 'cache_v': cache_v,
            'cache_logf': cache_logf, 'state_ssm': state_ssm, 'state_conv': state_conv,
            'page_table': page_table, 'w_in': w_in, 'b_fgate': b_fgate, 'conv_w': conv_w,
            'conv_b': conv_b, 'dt_bias': dt_bias, 'a_log': a_log, 'd_skip': d_skip,
            'ssm_norm_w': ssm_norm_w, 'w_br_att': w_br_att, 'w_br_ssm': w_br_ssm, 'w_out': w_out,
            'ln1_g': ln1_g, 'ln1_b': ln1_b, 'w_query': w_query, 'sub_keys': sub_keys,
            'expert_u': expert_u, 'expert_v': expert_v, 'ln2_g': ln2_g, 'ln2_b': ln2_b}


def reference(x_prompt, x_sample, cache_k, cache_v, cache_logf, state_ssm, state_conv, page_table,
              w_in, b_fgate, conv_w, conv_b, dt_bias, a_log, d_skip, ssm_norm_w,
              w_br_att, w_br_ssm, w_out, ln1_g, ln1_b, w_query, sub_keys, expert_u, expert_v,
              ln2_g, ln2_b):
    n_seq = x_prompt.shape[0]
    dec_b = x_sample.shape[0]
    xp, xs = x_prompt, x_sample
    kp_l, vp_l, fp_l, hp_l, cp_l = [], [], [], [], []
    ks_l, vs_l, fs_l, hs_l, cs_l = [], [], [], [], []
    for i in range(DEPTH):
        ssm_w = (conv_w[i], conv_b[i], dt_bias[i], a_log[i], d_skip[i], ssm_norm_w[i])
        merge_w = (w_br_att[i], w_br_ssm[i], w_out[i], ln1_g[i], ln1_b[i])
        peer_w = (w_query[i], sub_keys[i], expert_u[i], expert_v[i])

        q, k, v, logf, z, xbc, dt_raw, ga, gs = mixer_inputs(xp, w_in[i], b_fgate[i])
        att = fox_prompt(q, k, v, logf)
        buf0 = jnp.zeros((n_seq, CONV_WIDTH - 1, CONV_DIM), xp.dtype)
        h0 = jnp.zeros((n_seq, SSM_HEADS, SSM_HEAD_DIM, SSM_STATE), jnp.float32)
        ssm, conv_new, h_new = mamba_branch(z, xbc, dt_raw, buf0, h0, *ssm_w)
        xp = merge_branches(xp, att, ssm, ga, gs, *merge_w)
        xp = layer_norm(DEEPNORM_ALPHA * xp + peer(xp, *peer_w), ln2_g[i], ln2_b[i])
        kp_l.append(k); vp_l.append(v); fp_l.append(logf); hp_l.append(h_new); cp_l.append(conv_new)

        q, k, v, logf, z, xbc, dt_raw, ga, gs = mixer_inputs(xs, w_in[i], b_fgate[i])
        k_past = cache_k[i][page_table].reshape(dec_b, -1, ATT_HEADS, ATT_HEAD_DIM)
        v_past = cache_v[i][page_table].reshape(dec_b, -1, ATT_HEADS, ATT_HEAD_DIM)
        f_past = cache_logf[i][page_table].reshape(dec_b, -1, ATT_HEADS)
        att = fox_sample(q, k, v, logf, k_past, v_past, f_past)
        ssm, conv_new, h_new = mamba_branch(z, xbc, dt_raw, state_conv[i], state_ssm[i], *ssm_w)
        xs = merge_branches(xs, att, ssm, ga, gs, *merge_w)
        xs = layer_norm(DEEPNORM_ALPHA * xs + peer(xs, *peer_w), ln2_g[i], ln2_b[i])
        ks_l.append(k); vs_l.append(v); fs_l.append(logf); hs_l.append(h_new); cs_l.append(conv_new)

    k_prompt = jnp.stack(kp_l)
    v_prompt = jnp.stack(vp_l)
    logf_prompt = jnp.stack(fp_l)
    ssm_prompt = jnp.stack(hp_l)
    conv_prompt = jnp.stack(cp_l)
    k_sample = jnp.stack(ks_l)
    v_sample = jnp.stack(vs_l)
    logf_sample = jnp.stack(fs_l)
    ssm_sample = jnp.stack(hs_l)
    conv_sample = jnp.stack(cs_l)
    return (xp, xs, k_prompt, v_prompt, logf_prompt, ssm_prompt, conv_prompt,
            k_sample, v_sample, logf_sample, ssm_sample, conv_sample)
```

```python
import functools
import math
from typing import NamedTuple

import jax
import jax.numpy as jnp
from jax import lax
from jax.experimental import pallas as pl
from jax.experimental.pallas import tpu as pltpu

F32 = jnp.float32
BF16 = jnp.bfloat16
HIGHEST = lax.Precision.HIGHEST

LANES = 128
SUBLANES = 8
VMEM_LIMIT_BYTES = 56 * 1024 * 1024

HEAD_DIM = 64
SSM_STATE = 128
N_KEYS = 128
PK_HALF = 128
PK_TOPK = 16
PAGE = 128
CONV_WIDTH = 4
NORM_EPS = 1e-5
ATT_SCALE = HEAD_DIM ** -0.5
CHUNK = 128
HIST = SUBLANES

NT_DIMS = (((1,), (1,)), ((), ()))
TN_DIMS = (((0,), (0,)), ((), ()))


class Dims(NamedTuple):
    d_model: int
    att_heads: int
    ssm_groups: int
    peer_heads: int

    @property
    def d_att(self):
        return self.att_heads * HEAD_DIM

    @property
    def ssm_heads(self):
        return self.d_model // HEAD_DIM

    @property
    def gn(self):
        return self.ssm_groups * SSM_STATE

    @property
    def conv_dim(self):
        return self.d_model + 2 * self.gn

    @property
    def off(self):
        d, a, g = self.d_model, self.d_att, self.gn
        o = {"xs": 0, "z": d, "ga": 2 * d, "gs": 3 * d, "q": 4 * d, "k": 4 * d + a,
             "v": 4 * d + 2 * a, "b": 4 * d + 3 * a, "c": 4 * d + 3 * a + g}
        o["end"] = o["c"] + g
        return o


def _params(sem):
    return pltpu.CompilerParams(dimension_semantics=sem, vmem_limit_bytes=VMEM_LIMIT_BYTES)


def _log_sigmoid(x):
    return jnp.minimum(x, 0.0) - jnp.log1p(jnp.exp(-jnp.abs(x)))


def _softplus(x):
    return jnp.maximum(x, 0.0) + jnp.log1p(jnp.exp(-jnp.abs(x)))


def _silu(x):
    return x * jax.nn.sigmoid(x)


def _split3(x):
    hi = x.astype(BF16).astype(F32)
    r1 = x - hi
    mid = r1.astype(BF16).astype(F32)
    return hi, mid, r1 - mid


def _layer_norm(y, g, b):
    mu = jnp.mean(y, axis=-1, keepdims=True)
    d = y - mu
    var = jnp.mean(d * d, axis=-1, keepdims=True)
    return d * lax.rsqrt(var + NORM_EPS) * g + b


def _in_proj_kernel(x_ref, w_ref, ws_ref, o_ref, os_ref, xb_ref):
    @pl.when(pl.program_id(1) == 0)
    def _():
        xb = x_ref[...].astype(BF16)
        xb_ref[...] = xb
        os_ref[...] = jnp.dot(xb, ws_ref[...], preferred_element_type=F32)

    o_ref[...] = jnp.dot(xb_ref[...], w_ref[...], preferred_element_type=F32)


def in_proj(x2d, w_main, w_small, tm, tn):
    t, d = x2d.shape
    n = w_main.shape[1]
    return pl.pallas_call(
        _in_proj_kernel,
        grid=(t // tm, n // tn),
        in_specs=[pl.BlockSpec((tm, d), lambda i, j: (i, 0)),
                  pl.BlockSpec((d, tn), lambda i, j: (0, j)),
                  pl.BlockSpec((d, LANES), lambda i, j: (0, 0))],
        out_specs=[pl.BlockSpec((tm, tn), lambda i, j: (i, j)),
                   pl.BlockSpec((tm, LANES), lambda i, j: (i, 0))],
        out_shape=[jax.ShapeDtypeStruct((t, n), F32), jax.ShapeDtypeStruct((t, LANES), F32)],
        scratch_shapes=[pltpu.VMEM((tm, d), BF16)],
        compiler_params=_params(("parallel", "arbitrary")),
        name="in_proj",
    )(x2d, w_main, w_small)


def _gate_kernel(us_ref, bias_ref, logf_ref, c_ref, *, n_heads, blk):
    s = us_ref.shape[0]
    r = lax.broadcasted_iota(jnp.int32, (blk, blk), 0)
    c = lax.broadcasted_iota(jnp.int32, (blk, blk), 1)
    tri = (r >= c).astype(F32)
    carry = jnp.zeros((1, LANES), F32)
    for i in range(s // blk):
        rows = slice(i * blk, (i + 1) * blk)
        lf = _log_sigmoid(us_ref[rows, :] + bias_ref[...])
        logf_ref[rows, :] = lf[:, :n_heads]
        cs = jnp.dot(tri, lf, precision=HIGHEST, preferred_element_type=F32) + carry
        c_ref[rows, :] = cs
        carry = cs[blk - 1:blk, :]


def gate(u_small, bias128, n_batch, seq, n_heads):
    blk = min(256, seq)
    return pl.pallas_call(
        functools.partial(_gate_kernel, n_heads=n_heads, blk=blk),
        grid=(n_batch,),
        in_specs=[pl.BlockSpec((seq, LANES), lambda b: (b, 0)),
                  pl.BlockSpec((1, LANES), lambda b: (0, 0))],
        out_specs=[pl.BlockSpec((seq, n_heads), lambda b: (b, 0)),
                   pl.BlockSpec((seq, LANES), lambda b: (b, 0))],
        out_shape=[jax.ShapeDtypeStruct((n_batch * seq, n_heads), F32),
                   jax.ShapeDtypeStruct((n_batch * seq, LANES), F32)],
        compiler_params=_params(("parallel",)),
        name="gate",
    )(u_small, bias128)


def _fox_kernel(q_ref, k_ref, v_ref, c_ref, o_ref, qa_ref, ka_ref, vb_ref, *, tq):
    s_len = q_ref.shape[0]
    hp = pl.program_id(1)
    lane = lax.broadcasted_iota(jnp.int32, (1, LANES), 1)
    src = lax.broadcasted_iota(jnp.int32, (LANES, LANES), 0)
    dst = lax.broadcasted_iota(jnp.int32, (LANES, LANES), 1)
    cblk = c_ref[...]
    vb_ref[...] = v_ref[...].astype(BF16)
    q = q_ref[...] * ATT_SCALE
    k = k_ref[...]
    for hh in range(2):
        a0 = HEAD_DIM * (1 - hh)
        real = (lane >= HEAD_DIM * hh) & (lane < HEAD_DIM * (hh + 1))
        sel = ((src == 2 * hp + hh) & (dst >= a0) & (dst < a0 + 6)).astype(F32)
        ext = jnp.dot(cblk, sel, precision=HIGHEST, preferred_element_type=F32)
        hi, mid, lo = _split3(ext)
        is0 = (lane == a0) | (lane == a0 + 3)
        is1 = (lane == a0 + 1) | (lane == a0 + 4)
        parts = jnp.where(is0, hi, jnp.where(is1, mid, lo))
        in_c = (lane >= a0) & (lane < a0 + 3)
        in_1 = (lane >= a0 + 3) & (lane < a0 + 6)
        qa = jnp.where(real, q, jnp.where(in_c, parts, jnp.where(in_1, 1.0, 0.0)))
        ka = jnp.where(real, k, jnp.where(in_c, 1.0, jnp.where(in_1, -parts, 0.0)))
        qa_ref[hh] = qa.astype(BF16)
        ka_ref[hh] = ka.astype(BF16)

    row = lax.broadcasted_iota(jnp.int32, (tq, tq), 0)
    col = lax.broadcasted_iota(jnp.int32, (tq, tq), 1)
    causal = col <= row

    def q_body(qi, carry):
        q0 = pl.multiple_of(qi * tq, tq)
        outs = []
        for hh in range(2):
            qt = qa_ref[hh, pl.ds(q0, tq), :]

            def kv_step(k0, m, l, acc, masked, hh=hh, qt=qt):
                kt = ka_ref[hh, pl.ds(k0, tq), :]
                s = lax.dot_general(qt, kt, NT_DIMS, preferred_element_type=F32)
                if masked:
                    s = jnp.where(causal, s, -jnp.inf)
                m_new = jnp.maximum(m, jnp.max(s, axis=-1, keepdims=True))
                alpha = jnp.exp(m - m_new)
                p = jnp.exp(s - m_new)
                l = alpha * l + jnp.sum(p, axis=-1, keepdims=True)
                acc = alpha * acc + jnp.dot(p.astype(BF16), vb_ref[pl.ds(k0, tq), :],
                                            preferred_element_type=F32)
                return m_new, l, acc

            def kv_body(j, c, kv_step=kv_step):
                return kv_step(pl.multiple_of(j * tq, tq), *c, False)

            init = (jnp.full((tq, 1), -jnp.inf, F32), jnp.zeros((tq, 1), F32),
                    jnp.zeros((tq, LANES), F32))
            m, l, acc = lax.fori_loop(0, qi, kv_body, init)
            m, l, acc = kv_step(q0, m, l, acc, True)
            outs.append(acc / l)
        o_ref[pl.ds(q0, tq), :] = jnp.where(lane < HEAD_DIM, outs[0], outs[1])
        return carry

    lax.fori_loop(0, s_len // tq, q_body, 0)


def fox_prompt(u_main, c, dims, n_batch, seq):
    off = dims.off
    n_pairs = dims.att_heads // 2
    qb, kb, vb = (off[n] // LANES for n in ("q", "k", "v"))
    tq = min(256, seq)
    return pl.pallas_call(
        functools.partial(_fox_kernel, tq=tq),
        grid=(n_batch, n_pairs),
        in_specs=[pl.BlockSpec((seq, LANES), lambda b, h: (b, qb + h)),
                  pl.BlockSpec((seq, LANES), lambda b, h: (b, kb + h)),
                  pl.BlockSpec((seq, LANES), lambda b, h: (b, vb + h)),
                  pl.BlockSpec((seq, LANES), lambda b, h: (b, 0))],
        out_specs=pl.BlockSpec((seq, LANES), lambda b, h: (b, h)),
        out_shape=jax.ShapeDtypeStruct((n_batch * seq, dims.d_att), F32),
        scratch_shapes=[pltpu.VMEM((2, seq, LANES), BF16), pltpu.VMEM((2, seq, LANES), BF16),
                        pltpu.VMEM((seq, LANES), BF16)],
        compiler_params=_params(("parallel", "arbitrary")),
        name="fox_prompt",
    )(u_main, u_main, u_main, c)


def _fox_page_kernel(pt_ref, q_ref, kn_ref, vn_ref, us_ref, bias_ref, *rest, pps, n_heads, n_new):
    k_refs = rest[:pps]
    v_refs = rest[pps:2 * pps]
    lf_refs = rest[2 * pps:3 * pps]
    att_ref, logf_ref = rest[3 * pps:3 * pps + 2]
    qbd_ref, kb_ref, vb_ref, lfs_ref, m_ref, l_ref, acc_ref, suf_ref = rest[3 * pps + 2:]
    del pt_ref
    a = n_heads * HEAD_DIM
    nr = n_heads * n_new
    j = pl.program_id(1)
    lane = lax.broadcasted_iota(jnp.int32, (1, LANES), 1)
    row_h = lax.broadcasted_iota(jnp.int32, (nr, 1), 0) // n_new
    row_l = lax.broadcasted_iota(jnp.int32, (nr, 1), 0) % n_new

    def write_bias_tiles(rows, rev):
        hi, mid, lo = _split3(rev)
        ones = (lane >= n_heads) & (lane < n_heads + 3)
        kb_ref[rows, a:a + LANES] = jnp.where(ones, 1.0, hi).astype(BF16)
        kb_ref[rows, a + LANES:a + 2 * LANES] = mid.astype(BF16)
        kb_ref[rows, a + 2 * LANES:a + 3 * LANES] = lo.astype(BF16)

    @pl.when(j == 0)
    def _init():
        lfn = _log_sigmoid(us_ref[...] + bias_ref[...])
        logf_ref[...] = lfn[:, :n_heads]
        lfn = jnp.where(lane < n_heads, lfn, 0.0)
        rows = lax.broadcasted_iota(jnp.int32, (n_new, LANES), 0)
        cn = jnp.zeros((n_new, LANES), F32)
        for r in range(n_new):
            cn = cn + jnp.where(rows >= r, lfn[r:r + 1, :], 0.0)
        q = q_ref[...] * ATT_SCALE
        qt = jnp.concatenate([q] * n_heads, axis=0)
        col_h = lax.broadcasted_iota(jnp.int32, (nr, a), 1) // HEAD_DIM
        qbd_ref[:, 0:a] = jnp.where(col_h == row_h, qt, 0.0).astype(BF16)
        onehot = lane == row_h
        cnt = jnp.concatenate([cn] * n_heads, axis=0)
        cn_col = jnp.sum(jnp.where(onehot, cnt, 0.0), axis=1, keepdims=True)
        hi, mid, lo = _split3(cn_col)
        oh = onehot.astype(F32)
        t_a = jnp.where(lane == n_heads, hi,
                        jnp.where(lane == n_heads + 1, mid, jnp.where(lane == n_heads + 2, lo, oh)))
        qbd_ref[:, a:a + LANES] = t_a.astype(BF16)
        qbd_ref[:, a + LANES:a + 2 * LANES] = oh.astype(BF16)
        qbd_ref[:, a + 2 * LANES:a + 3 * LANES] = oh.astype(BF16)
        pad = PAGE - n_new
        kb_ref[0:PAGE, 0:a] = jnp.concatenate(
            [kn_ref[...], jnp.zeros((pad, a), F32)], axis=0).astype(BF16)
        vb_ref[0:PAGE, :] = jnp.concatenate(
            [vn_ref[...], jnp.zeros((pad, a), F32)], axis=0).astype(BF16)
        write_bias_tiles(slice(0, PAGE), jnp.concatenate([-cn, jnp.zeros((pad, LANES), F32)], axis=0))
        s = lax.dot_general(qbd_ref[...], kb_ref[0:PAGE, :], NT_DIMS, preferred_element_type=F32)
        key = lax.broadcasted_iota(jnp.int32, (nr, PAGE), 1)
        s = jnp.where((key < n_new) & (key <= row_l), s, -jnp.inf)
        m = jnp.max(s, axis=-1, keepdims=True)
        p = jnp.exp(s - m)
        m_ref[...] = m
        l_ref[...] = jnp.sum(p, axis=-1, keepdims=True)
        acc_ref[...] = jnp.dot(p.astype(BF16), vb_ref[0:PAGE, :], preferred_element_type=F32)
        suf_ref[...] = jnp.zeros((1, LANES), F32)
        lfs_ref[...] = jnp.zeros((PAGE, LANES), F32)

    rowi = lax.broadcasted_iota(jnp.int32, (PAGE, LANES), 0)
    suf = suf_ref[...]
    for i in range(pps):
        rows = slice(i * PAGE, (i + 1) * PAGE)
        kb_ref[rows, 0:a] = k_refs[i][0].astype(BF16)
        vb_ref[rows, :] = v_refs[i][0].astype(BF16)
        lfs_ref[:, 0:n_heads] = lf_refs[i][0]
        x = lfs_ref[...]
        incl = x
        d = 1
        while d < PAGE:
            incl = incl + jnp.where(rowi + d < PAGE, pltpu.roll(incl, PAGE - d, 0), 0.0)
            d *= 2
        write_bias_tiles(rows, incl - x + suf)
        suf = suf + incl[0:1, :]
    suf_ref[...] = suf

    s = lax.dot_general(qbd_ref[...], kb_ref[...], NT_DIMS, preferred_element_type=F32)
    m_old = m_ref[...]
    m_new = jnp.maximum(m_old, jnp.max(s, axis=-1, keepdims=True))
    alpha = jnp.exp(m_old - m_new)
    p = jnp.exp(s - m_new)
    m_ref[...] = m_new
    l_ref[...] = alpha * l_ref[...] + jnp.sum(p, axis=-1, keepdims=True)
    acc_ref[...] = alpha * acc_ref[...] + jnp.dot(p.astype(BF16), vb_ref[...],
                                                  preferred_element_type=F32)

    @pl.when(j == pl.num_programs(1) - 1)
    def _fin():
        o = acc_ref[...] / l_ref[...]
        col_h = lax.broadcasted_iota(jnp.int32, (nr, a), 1) // HEAD_DIM
        o = jnp.where(col_h == row_h, o, 0.0)
        out = o[0:n_new, :]
        for h in range(1, n_heads):
            out = out + o[h * n_new:(h + 1) * n_new, :]
        att_ref[...] = out


def fox_sample(u_main, u_small, bias128, cache_k, cache_v, cache_logf, page_table, dims, pps):
    nb, n_pages = page_table.shape
    n_new = u_main.shape[0] // nb
    a = dims.d_att
    off = dims.off
    qb, kb, vb = (off[n] // a for n in ("q", "k", "v"))
    nr = dims.att_heads * n_new
    kw = a + 3 * LANES

    def page_map(i):
        return lambda b, j, pt: (pt[b * n_pages + n_pages - 1 - (j * pps + i)], 0, 0)

    in_specs = [pl.BlockSpec((n_new, a), lambda b, j, pt: (b, qb)),
                pl.BlockSpec((n_new, a), lambda b, j, pt: (b, kb)),
                pl.BlockSpec((n_new, a), lambda b, j, pt: (b, vb)),
                pl.BlockSpec((n_new, LANES), lambda b, j, pt: (b, 0)),
                pl.BlockSpec((1, LANES), lambda b, j, pt: (0, 0))]
    in_specs += [pl.BlockSpec((1, PAGE, a), page_map(i)) for i in range(pps)]
    in_specs += [pl.BlockSpec((1, PAGE, a), page_map(i)) for i in range(pps)]
    in_specs += [pl.BlockSpec((1, PAGE, dims.att_heads), page_map(i)) for i in range(pps)]
    grid_spec = pltpu.PrefetchScalarGridSpec(
        num_scalar_prefetch=1,
        grid=(nb, n_pages // pps),
        in_specs=in_specs,
        out_specs=[pl.BlockSpec((n_new, a), lambda b, j, pt: (b, 0)),
                   pl.BlockSpec((n_new, dims.att_heads), lambda b, j, pt: (b, 0))],
        scratch_shapes=[pltpu.VMEM((nr, kw), BF16), pltpu.VMEM((pps * PAGE, kw), BF16),
                        pltpu.VMEM((pps * PAGE, a), BF16), pltpu.VMEM((PAGE, LANES), F32),
                        pltpu.VMEM((nr, 1), F32), pltpu.VMEM((nr, 1), F32),
                        pltpu.VMEM((nr, a), F32), pltpu.VMEM((1, LANES), F32)])
    return pl.pallas_call(
        functools.partial(_fox_page_kernel, pps=pps, n_heads=dims.att_heads, n_new=n_new),
        grid_spec=grid_spec,
        out_shape=[jax.ShapeDtypeStruct((nb * n_new, a), F32),
                   jax.ShapeDtypeStruct((nb * n_new, dims.att_heads), F32)],
        compiler_params=_params(("parallel", "arbitrary")),
        name="fox_sample",
    )(page_table.reshape(-1), u_main, u_main, u_main, u_small, bias128,
      *([cache_k] * pps), *([cache_v] * pps), *([cache_logf] * pps))


def _mamba_kernel(*refs, tl, dims, has_init):
    if has_init:
        (xs_ref, z_ref, b_ref, c_ref, us_ref, cw_ref, cb_ref, dtb_ref, alog_ref, dsk_ref, nw_ref,
         h0_ref, conv0_ref, y_ref, hout_ref, convout_ref, xp_ref, cv_ref, h_ref, ysc_ref) = refs
    else:
        (xs_ref, z_ref, b_ref, c_ref, us_ref, cw_ref, cb_ref, dtb_ref, alog_ref, dsk_ref, nw_ref,
         y_ref, hout_ref, convout_ref, xp_ref, cv_ref, h_ref, ysc_ref) = refs
    d = dims.d_model
    gn = dims.gn
    cw = dims.conv_dim
    heads = dims.ssm_heads
    per_group = heads // dims.ssm_groups
    dt_off = dims.att_heads
    c_idx = pl.program_id(1)
    last = pl.num_programs(1) - 1

    @pl.when(c_idx == 0)
    def _init():
        xp_ref[0:HIST, :] = jnp.zeros((HIST, cw), F32)
        if has_init:
            xp_ref[HIST - (CONV_WIDTH - 1):HIST, :] = conv0_ref[0]
            h_ref[...] = h0_ref[0]
        else:
            h_ref[...] = jnp.zeros(h_ref.shape, F32)
        if tl < CHUNK:
            xp_ref[HIST + tl:HIST + CHUNK, :] = jnp.zeros((CHUNK - tl, cw), F32)

    xp_ref[HIST:HIST + tl, 0:d] = xs_ref[...]
    xp_ref[HIST:HIST + tl, d:d + gn] = b_ref[...]
    xp_ref[HIST:HIST + tl, d + gn:cw] = c_ref[...]

    lc = 512
    for c0 in range(0, cw, lc):
        cols = slice(c0, c0 + lc)
        acc = cb_ref[:, cols] + cw_ref[0:1, cols] * xp_ref[HIST - 3:HIST - 3 + CHUNK, cols]
        for jj in range(1, CONV_WIDTH):
            acc = acc + cw_ref[jj:jj + 1, cols] * xp_ref[HIST - 3 + jj:HIST - 3 + jj + CHUNK, cols]
        cv_ref[:, cols] = _silu(acc)

    @pl.when(c_idx == last)
    def _conv_out():
        convout_ref[0] = xp_ref[HIST + tl - (CONV_WIDTH - 1):HIST + tl, :]

    xp_ref[0:HIST, :] = xp_ref[CHUNK:CHUNK + HIST, :]

    lane = lax.broadcasted_iota(jnp.int32, (1, LANES), 1)
    rowc = lax.broadcasted_iota(jnp.int32, (CHUNK, 1), 0)
    us = us_ref[...]
    if tl < CHUNK:
        us = jnp.concatenate([us, jnp.zeros((CHUNK - tl, LANES), F32)], axis=0)
    is_dt = (lane >= dt_off) & (lane < dt_off + heads) & (rowc < tl)
    dt = jnp.where(is_dt, _softplus(us + dtb_ref[...]), 0.0)
    dta = dt * (-jnp.exp(alog_ref[...]))
    r = lax.broadcasted_iota(jnp.int32, (CHUNK, CHUNK), 0)
    c = lax.broadcasted_iota(jnp.int32, (CHUNK, CHUNK), 1)
    lower = r >= c
    cs = jnp.dot(lower.astype(F32), dta, precision=HIGHEST, preferred_element_type=F32)
    cs_t = cs.T
    ecs = jnp.exp(cs)
    lt64 = lane < HEAD_DIM
    rlt64 = rowc < HEAD_DIM

    for g in range(dims.ssm_groups):
        bg = cv_ref[:, d + g * SSM_STATE:d + (g + 1) * SSM_STATE].astype(BF16)
        cg = cv_ref[:, d + gn + g * SSM_STATE:d + gn + (g + 1) * SSM_STATE].astype(BF16)
        cbm = lax.dot_general(cg, bg, NT_DIMS, preferred_element_type=F32)
        for pi in range(per_group // 2):
            pair = g * (per_group // 2) + pi
            l0 = dt_off + 2 * pair
            cols = slice(pair * LANES, (pair + 1) * LANES)
            xsp = cv_ref[:, cols]
            cs0, cs1 = cs[:, l0:l0 + 1], cs[:, l0 + 1:l0 + 2]
            xc = xsp * jnp.where(lt64, dt[:, l0:l0 + 1], dt[:, l0 + 1:l0 + 2])
            xcb = xc.astype(BF16)
            w0 = (cbm * jnp.where(lower, jnp.exp(cs0 - cs_t[l0:l0 + 1, :]), 0.0)).astype(BF16)
            w1 = (cbm * jnp.where(lower, jnp.exp(cs1 - cs_t[l0 + 1:l0 + 2, :]), 0.0)).astype(BF16)
            yd = jnp.where(lt64, jnp.dot(w0, xcb, preferred_element_type=F32),
                           jnp.dot(w1, xcb, preferred_element_type=F32))
            cl0, cl1 = cs[CHUNK - 1:CHUNK, l0:l0 + 1], cs[CHUNK - 1:CHUNK, l0 + 1:l0 + 2]
            dend = jnp.where(lt64, jnp.exp(cl0 - cs0), jnp.exp(cl1 - cs1))
            st = lax.dot_general((xc * dend).astype(BF16), bg, TN_DIMS,
                                 preferred_element_type=F32)
            hprev = h_ref[cols, :]
            yo = lax.dot_general(cg, hprev.astype(BF16), NT_DIMS, preferred_element_type=F32)
            y = yd + yo * jnp.where(lt64, ecs[:, l0:l0 + 1], ecs[:, l0 + 1:l0 + 2])
            h_ref[cols, :] = hprev * jnp.where(rlt64, jnp.exp(cl0), jnp.exp(cl1)) + st
            ysc_ref[:, cols] = y + dsk_ref[:, cols] * xsp

    @pl.when(c_idx == last)
    def _h_out():
        hout_ref[0] = h_ref[...]

    gw = d // dims.ssm_groups
    for g in range(dims.ssm_groups):
        cols = slice(g * gw, (g + 1) * gw)
        yg = ysc_ref[0:tl, cols] * _silu(z_ref[:, cols])
        ms = jnp.mean(yg * yg, axis=-1, keepdims=True)
        y_ref[:, cols] = yg * lax.rsqrt(ms + NORM_EPS) * nw_ref[:, cols]


def mamba(u_main, u_small, conv_w, conv_b, dtb128, alog128, dsk_lanes, norm_w, dims, n_batch, seq,
          h0=None, conv0=None):
    d, gn, cw = dims.d_model, dims.gn, dims.conv_dim
    off = dims.off
    tl = min(seq, CHUNK)
    nc = seq // tl
    has_init = h0 is not None
    hrows = dims.ssm_heads * HEAD_DIM
    row = lambda b, c: b * nc + c
    const = lambda b, c: (0, 0)
    in_specs = [pl.BlockSpec((tl, d), lambda b, c: (row(b, c), off["xs"] // d)),
                pl.BlockSpec((tl, d), lambda b, c: (row(b, c), off["z"] // d)),
                pl.BlockSpec((tl, gn), lambda b, c: (row(b, c), off["b"] // gn)),
                pl.BlockSpec((tl, gn), lambda b, c: (row(b, c), off["c"] // gn)),
                pl.BlockSpec((tl, LANES), lambda b, c: (row(b, c), 0)),
                pl.BlockSpec((CONV_WIDTH, cw), const), pl.BlockSpec((1, cw), const),
                pl.BlockSpec((1, LANES), const), pl.BlockSpec((1, LANES), const),
                pl.BlockSpec((1, d), const), pl.BlockSpec((1, d), const)]
    args = [u_main, u_main, u_main, u_main, u_small, conv_w, conv_b, dtb128, alog128, dsk_lanes,
            norm_w]
    if has_init:
        in_specs += [pl.BlockSpec((1, hrows, SSM_STATE), lambda b, c: (b, 0, 0)),
                     pl.BlockSpec((1, CONV_WIDTH - 1, cw), lambda b, c: (b, 0, 0))]
        args += [h0, conv0]
    return pl.pallas_call(
        functools.partial(_mamba_kernel, tl=tl, dims=dims, has_init=has_init),
        grid=(n_batch, nc),
        in_specs=in_specs,
        out_specs=[pl.BlockSpec((tl, d), lambda b, c: (row(b, c), 0)),
                   pl.BlockSpec((1, hrows, SSM_STATE), lambda b, c: (b, 0, 0)),
                   pl.BlockSpec((1, CONV_WIDTH - 1, cw), lambda b, c: (b, 0, 0))],
        out_shape=[jax.ShapeDtypeStruct((n_batch * seq, d), F32),
                   jax.ShapeDtypeStruct((n_batch, hrows, SSM_STATE), F32),
                   jax.ShapeDtypeStruct((n_batch, CONV_WIDTH - 1, cw), F32)],
        scratch_shapes=[pltpu.VMEM((HIST + CHUNK, cw), F32), pltpu.VMEM((CHUNK, cw), F32),
                        pltpu.VMEM((hrows, SSM_STATE), F32), pltpu.VMEM((CHUNK, d), F32)],
        compiler_params=_params(("parallel", "arbitrary")),
        name="mamba",
    )(*args)


def _merge_kernel(x_ref, att_ref, ssm_ref, ga_ref, gs_ref, wba_ref, wbs_ref, wo_ref, g_ref, b_ref,
                  o_ref, *, alpha):
    pa = jnp.dot(att_ref[...].astype(BF16), wba_ref[...], preferred_element_type=F32)
    ps = jnp.dot(ssm_ref[...].astype(BF16), wbs_ref[...], preferred_element_type=F32)
    mixed = jax.nn.sigmoid(ga_ref[...]) * pa + jax.nn.sigmoid(gs_ref[...]) * ps
    y = alpha * x_ref[...] + jnp.dot(mixed.astype(BF16), wo_ref[...], preferred_element_type=F32)
    o_ref[...] = _layer_norm(y, g_ref[...], b_ref[...])


def merge(x2d, att, ssm, u_main, wba, wbs, wo, g, b, dims, alpha, tm):
    t, d = x2d.shape
    a = dims.d_att
    off = dims.off
    const = lambda i: (0, 0)
    once = pl.Buffered(1)
    return pl.pallas_call(
        functools.partial(_merge_kernel, alpha=alpha),
        grid=(t // tm,),
        in_specs=[pl.BlockSpec((tm, d), lambda i: (i, 0)),
                  pl.BlockSpec((tm, a), lambda i: (i, 0)),
                  pl.BlockSpec((tm, d), lambda i: (i, 0)),
                  pl.BlockSpec((tm, d), lambda i: (i, off["ga"] // d)),
                  pl.BlockSpec((tm, d), lambda i: (i, off["gs"] // d)),
                  pl.BlockSpec((a, d), const, pipeline_mode=once),
                  pl.BlockSpec((d, d), const, pipeline_mode=once),
                  pl.BlockSpec((d, d), const, pipeline_mode=once),
                  pl.BlockSpec((1, d), const), pl.BlockSpec((1, d), const)],
        out_specs=pl.BlockSpec((tm, d), lambda i: (i, 0)),
        out_shape=jax.ShapeDtypeStruct((t, d), F32),
        compiler_params=_params(("parallel",)),
        name="merge",
    )(x2d, att, ssm, u_main, u_main, wba, wbs, wo, g, b)


def _peer_kernel(x_ref, wqt_ref, keys_ref, u_ref, vt_ref, g_ref, b_ref, o_ref,
                 xb_ref, qt_ref, s1_ref, s2_ref, e1_ref, e2_ref, tau_ref, top_ref, cand_ref,
                 p_ref, acc_ref, *, n_heads, alpha):
    e = pl.program_id(1)
    eb = u_ref.shape[0]
    n_blk = eb // N_KEYS
    neg_inf = -jnp.inf

    @pl.when(e == 0)
    def _route():
        xb = x_ref[...].astype(BF16)
        xb_ref[...] = xb
        qt_ref[...] = lax.dot_general(wqt_ref[...], xb, NT_DIMS,
                                      preferred_element_type=F32).astype(BF16)

        def head_body(h, carry):
            for half, s_ref in ((0, s1_ref), (1, s2_ref)):
                r0 = pl.multiple_of((2 * h + half) * PK_HALF, PK_HALF)
                s = jnp.dot(keys_ref[h, half], qt_ref[pl.ds(r0, PK_HALF), :],
                            preferred_element_type=F32)
                s_ref[h] = s
                w = s
                for r in range(PK_TOPK):
                    m = jnp.max(w, axis=0, keepdims=True)
                    top_ref[half, r:r + 1, :] = m
                    w = jnp.where(w == m, neg_inf, w)
            ta = top_ref[0]
            tb = top_ref[1]
            cand_ref[0:16, :] = ta[0:1, :] + tb
            for i in range(1, 8):
                cand_ref[8 + 8 * i:16 + 8 * i, :] = ta[i:i + 1, :] + tb[0:8, :]
            cand_ref[72:80, :] = ta[8:16, :] + tb[0:1, :]
            cand = cand_ref[...]
            w = cand
            for r in range(PK_TOPK):
                m = jnp.max(w, axis=0, keepdims=True)
                w = jnp.where(w == m, neg_inf, w)
            tau = m
            cmax = ta[0:1, :] + tb[0:1, :]
            z = jnp.sum(jnp.where(cand >= tau, jnp.exp(cand - cmax), 0.0), axis=0, keepdims=True)
            tau_ref[pl.ds(h, 1), :] = tau
            e1_ref[h] = jnp.exp(s1_ref[h] - ta[0:1, :]) / z
            e2_ref[h] = jnp.exp(s2_ref[h] - tb[0:1, :])
            return carry

        lax.fori_loop(0, n_heads, head_body, 0)
        acc_ref[...] = jnp.zeros(acc_ref.shape, F32)

    ht = lax.dot_general(u_ref[...], xb_ref[...], NT_DIMS, preferred_element_type=F32)
    act = 0.5 * ht * (1.0 + lax.erf(ht * (2.0 ** -0.5)))
    for ii in range(n_blk):
        i1 = e * n_blk + ii
        gate = None
        for h in range(n_heads):
            sc = s1_ref[h, pl.ds(i1, 1), :] + s2_ref[h]
            gh = jnp.where(sc >= tau_ref[h:h + 1, :], e2_ref[h], 0.0) * e1_ref[h, pl.ds(i1, 1), :]
            gate = gh if gate is None else gate + gh
        rows = slice(ii * N_KEYS, (ii + 1) * N_KEYS)
        p_ref[rows, :] = (gate * act[rows, :]).astype(BF16)
    acc_ref[...] += jnp.dot(vt_ref[...], p_ref[...], preferred_element_type=F32)

    @pl.when(e == pl.num_programs(1) - 1)
    def _fin():
        y = alpha * x_ref[...] + acc_ref[...].T
        o_ref[...] = _layer_norm(y, g_ref[...], b_ref[...])


def peer(x2d, wqt, keys, u, vt, g, b, dims, alpha, tt, eb):
    t, d = x2d.shape
    nh = dims.peer_heads
    n_exp = u.shape[0]
    const2 = lambda i, e: (0, 0)
    once = pl.Buffered(1)
    return pl.pallas_call(
        functools.partial(_peer_kernel, n_heads=nh, alpha=alpha),
        grid=(t // tt, n_exp // eb),
        in_specs=[pl.BlockSpec((tt, d), lambda i, e: (i, 0), pipeline_mode=once),
                  pl.BlockSpec((nh * 2 * PK_HALF, d), const2, pipeline_mode=once),
                  pl.BlockSpec((nh, 2, N_KEYS, PK_HALF), lambda i, e: (0, 0, 0, 0),
                               pipeline_mode=once),
                  pl.BlockSpec((eb, d), lambda i, e: (e, 0)),
                  pl.BlockSpec((d, eb), lambda i, e: (0, e)),
                  pl.BlockSpec((1, d), const2), pl.BlockSpec((1, d), const2)],
        out_specs=pl.BlockSpec((tt, d), lambda i, e: (i, 0), pipeline_mode=once),
        out_shape=jax.ShapeDtypeStruct((t, d), F32),
        scratch_shapes=[pltpu.VMEM((tt, d), BF16),
                        pltpu.VMEM((nh * 2 * PK_HALF, tt), BF16),
                        pltpu.VMEM((nh, N_KEYS, tt), F32), pltpu.VMEM((nh, N_KEYS, tt), F32),
                        pltpu.VMEM((nh, N_KEYS, tt), F32), pltpu.VMEM((nh, N_KEYS, tt), F32),
                        pltpu.VMEM((nh, tt), F32), pltpu.VMEM((2, PK_TOPK, tt), F32),
                        pltpu.VMEM((80, tt), F32), pltpu.VMEM((eb, tt), BF16),
                        pltpu.VMEM((d, tt), F32)],
        compiler_params=_params(("parallel", "arbitrary")),
        name="peer",
    )(x2d, wqt, keys, u, vt, g, b)


def _lanes(v, offset):
    return jnp.zeros((1, LANES), F32).at[0, offset:offset + v.shape[0]].set(v)


def _tile(n, pref):
    for t in pref:
        if n % t == 0:
            return t
    return n


def hybrid_step(dims, x_prompt, x_sample, cache_k, cache_v, cache_logf, state_ssm, state_conv,
                page_table, w_in, b_fgate, conv_w, conv_b, dt_bias, a_log, d_skip, ssm_norm_w,
                w_br_att, w_br_ssm, w_out, ln1_g, ln1_b, w_query, sub_keys, expert_u, expert_v,
                ln2_g, ln2_b, pps=4):
    depth = w_in.shape[0]
    assert depth == 1
    d, a, gn = dims.d_model, dims.d_att, dims.gn
    heads, ssm_heads = dims.att_heads, dims.ssm_heads
    nb, seq, _ = x_prompt.shape
    db, n_new, _ = x_sample.shape
    alpha = float((2 * depth) ** 0.25)
    off = dims.off
    assert off["q"] % a == 0 and off["b"] % gn == 0 and heads + ssm_heads <= LANES

    w = w_in[0]
    bounds = [0]
    for sz in (a, a, a, heads, d, dims.conv_dim, ssm_heads, d, d):
        bounds.append(bounds[-1] + sz)
    wq, wk, wv, wf, wz, wxbc, wdt, wga, wgs = (w[:, bounds[i]:bounds[i + 1]] for i in range(9))
    w_main = jnp.concatenate(
        [wxbc[:, :d], wz, wga, wgs, wq, wk, wv, wxbc[:, d:d + gn], wxbc[:, d + gn:]], axis=1
    ).astype(BF16)
    w_small = jnp.pad(jnp.concatenate([wf, wdt], axis=1),
                      ((0, 0), (0, LANES - heads - ssm_heads))).astype(BF16)
    fb128 = _lanes(b_fgate[0], 0)
    dtb128 = _lanes(dt_bias[0], heads)
    alog128 = _lanes(a_log[0], heads)
    dsk_lanes = jnp.repeat(d_skip[0], HEAD_DIM)[None, :]
    nw = ssm_norm_w[0][None, :]
    cb = conv_b[0][None, :]
    wba, wbs, wo = w_br_att[0].astype(BF16), w_br_ssm[0].astype(BF16), w_out[0].astype(BF16)
    wqt = w_query[0].T.astype(BF16)
    keys = sub_keys[0].astype(BF16)
    eu = expert_u[0].astype(BF16)
    evt = expert_v[0].T.astype(BF16)
    n_pool = cache_k.shape[1]
    ck = cache_k[0].reshape(n_pool, PAGE, a)
    cv = cache_v[0].reshape(n_pool, PAGE, a)
    cl = cache_logf[0]

    def shared_tail(x2d, att, ssm, u_main, tm, tt):
        x1 = merge(x2d, att, ssm, u_main, wba, wbs, wo, ln1_g, ln1_b, dims, alpha, tm)
        return peer(x1, wqt, keys, eu, evt, ln2_g, ln2_b, dims, alpha, tt, eb=min(512, eu.shape[0]))

    tp = nb * seq
    xp2 = x_prompt.reshape(tp, d)
    up, usp = in_proj(xp2, w_main, w_small, _tile(tp, (1024, 512, 256)), 512)
    logf_p, c_p = gate(usp, fb128, nb, seq, heads)
    att_p = fox_prompt(up, c_p, dims, nb, seq)
    ssm_p, h_p, conv_p = mamba(up, usp, conv_w[0], cb, dtb128, alog128, dsk_lanes, nw, dims, nb, seq)
    y_p = shared_tail(xp2, att_p, ssm_p, up, _tile(tp, (256,)), _tile(tp, (512, 256)))

    ts = db * n_new
    xs2 = x_sample.reshape(ts, d)
    us_, uss = in_proj(xs2, w_main, w_small, _tile(ts, (256,)), 512)
    att_s, logf_s = fox_sample(us_, uss, fb128, ck, cv, cl, page_table, dims, pps)
    ssm_s, h_s, conv_s = mamba(us_, uss, conv_w[0], cb, dtb128, alog128, dsk_lanes, nw, dims, db,
                               n_new, h0=state_ssm[0].reshape(db, ssm_heads * HEAD_DIM, SSM_STATE),
                               conv0=state_conv[0])
    y_s = shared_tail(xs2, att_s, ssm_s, us_, _tile(ts, (256,)), _tile(ts, (256,)))

    hs = (heads, HEAD_DIM)
    ss = (ssm_heads, HEAD_DIM, SSM_STATE)
    return (y_p.reshape(nb, seq, d), y_s.reshape(db, n_new, d),
            up[:, off["k"]:off["k"] + a].reshape(1, nb, seq, *hs),
            up[:, off["v"]:off["v"] + a].reshape(1, nb, seq, *hs),
            logf_p.reshape(1, nb, seq, heads),
            h_p.reshape(1, nb, *ss), conv_p[None],
            us_[:, off["k"]:off["k"] + a].reshape(1, db, n_new, *hs),
            us_[:, off["v"]:off["v"] + a].reshape(1, db, n_new, *hs),
            logf_s.reshape(1, db, n_new, heads),
            h_s.reshape(1, db, *ss), conv_s[None])


_DIMS = Dims(d_model=2048, att_heads=16, ssm_groups=8, peer_heads=8)


def kernel(x_prompt, x_sample, cache_k, cache_v, cache_logf, state_ssm, state_conv, page_table,
           w_in, b_fgate, conv_w, conv_b, dt_bias, a_log, d_skip, ssm_norm_w, w_br_att, w_br_ssm,
           w_out, ln1_g, ln1_b, w_query, sub_keys, expert_u, expert_v, ln2_g, ln2_b):
    return hybrid_step(_DIMS, x_prompt, x_sample, cache_k, cache_v, cache_logf, state_ssm,
                       state_conv, page_table, w_in, b_fgate, conv_w, conv_b, dt_bias, a_log,
                       d_skip, ssm_norm_w, w_br_att, w_br_ssm, w_out, ln1_g, ln1_b, w_query,
                       sub_keys, expert_u, expert_v, ln2_g, ln2_b)
```
